```python
import jax, jax.numpy as jnp
from jax import lax
import numpy as np

D_MODEL = 1024
BATCH = 8
SEQ = 2048
DEPTH = 4

CHUNK = 64
POOL_WINDOWS = (2, 4, 8, 16)
POOL_GROUPS = len(POOL_WINDOWS)
POOL_WIDTH = D_MODEL // 2
POOL_GROUP_DIM = POOL_WIDTH // POOL_GROUPS
POOL_MAX_WIN = max(POOL_WINDOWS)
CONV_WIDTH = D_MODEL // 2
CONV_KERNEL = 31
EVEN_IN = POOL_WIDTH + 2 * CONV_WIDTH
EVEN_MIX = POOL_WIDTH + CONV_WIDTH
SGU_WIDTH = 2 * D_MODEL
SGU_HEADS = 8
SGU_HEAD_DIM = SGU_WIDTH // SGU_HEADS
SGU_CHUNK = 128
N_EXPERTS = 32
TOP_K = 4
D_EXPERT = D_MODEL
SWIGLU_LIMIT = 7.0
SWIGLU_ALPHA = 1.702
MOE_BLOCK = 128
DN_ALPHA = (2 * DEPTH) ** 0.25
DN_BETA = (8 * DEPTH) ** -0.25
LN_EPS = 1e-5
N_EVEN = (DEPTH + 1) // 2
N_ODD = DEPTH // 2

kernel_name = "hybrid_pool_conv_gmlp_moe_deepnorm"


def layer_norm(x, g, b):
    xf = x.astype(jnp.float32)
    mu = xf.mean(-1, keepdims=True)
    var = jnp.square(xf - mu).mean(-1, keepdims=True)
    return ((xf - mu) * lax.rsqrt(var + LN_EPS) * g + b).astype(x.dtype)


def multiscale_pool(p, w, b, scale):
    bsz, s, _ = p.shape
    groups = p.astype(jnp.float32).reshape(bsz, s, POOL_GROUPS, POOL_GROUP_DIM)
    cs = jnp.pad(jnp.cumsum(groups, axis=1), ((0, 0), (POOL_MAX_WIN, 0), (0, 0), (0, 0)))
    pos = jnp.arange(s)
    outs = []
    for g, win in enumerate(POOL_WINDOWS):
        window_sum = (cs[:, POOL_MAX_WIN:POOL_MAX_WIN + s, g]
                      - cs[:, POOL_MAX_WIN - win:POOL_MAX_WIN - win + s, g])
        count = jnp.minimum(pos + 1, win).astype(jnp.float32)[None, :, None]
        outs.append(window_sum / count - groups[:, :, g])
    pooled = jnp.stack(outs, axis=2).astype(p.dtype)
    mixed = jnp.einsum('bsgc,gcd->bsgd', pooled, w) + b
    return mixed.reshape(bsz, s, POOL_WIDTH) * scale


def conformer_conv(a, gate, w_dw, b_dw, ln_g, ln_b):
    glu = a * jax.nn.sigmoid(gate)
    padded = jnp.pad(glu, ((0, 0), (CONV_KERNEL - 1, 0), (0, 0)))
    y = lax.conv_general_dilated(
        padded, w_dw[:, None, :], window_strides=(1,), padding='VALID',
        dimension_numbers=('NWC', 'WIO', 'NWC'), feature_group_count=CONV_WIDTH) + b_dw
    return jax.nn.silu(layer_norm(y, ln_g, ln_b))


def even_mixer(x, w_in, b_in, pool_w, pool_b, pool_scale, conv_w, conv_b,
               conv_ln_g, conv_ln_b, w_out, b_out):
    h = jnp.einsum('bsd,de->bse', x, w_in) + b_in
    p, a, gate = jnp.split(h, [POOL_WIDTH, POOL_WIDTH + CONV_WIDTH], axis=-1)
    y_pool = multiscale_pool(p, pool_w, pool_b, pool_scale)
    y_conv = conformer_conv(a, gate, conv_w, conv_b, conv_ln_g, conv_ln_b)
    y = jnp.concatenate([y_pool, y_conv], axis=-1)
    return jnp.einsum('bse,ed->bsd', y, w_out) + b_out


def odd_mixer(x, w_in, b_in, ln_g, ln_b, w_s, b_s, w_out, b_out):
    bsz, s, _ = x.shape
    h = jax.nn.gelu(jnp.einsum('bsd,de->bse', x, w_in) + b_in, approximate=False)
    u, v = jnp.split(h, 2, axis=-1)
    v = layer_norm(v, ln_g, ln_b)
    v = v.reshape(bsz, s // SGU_CHUNK, SGU_CHUNK, SGU_HEADS, SGU_HEAD_DIM)
    blk = jnp.arange(SGU_CHUNK) // CHUNK
    mask = blk[:, None] >= blk[None, :]
    w_masked = jnp.where(mask[None], w_s, 0.0)
    sv = jnp.einsum('hij,bnjhc->bnihc', w_masked, v) + b_s.T[:, :, None]
    gated = u * sv.reshape(bsz, s, SGU_WIDTH)
    return jnp.einsum('bse,ed->bsd', gated, w_out) + b_out


def moe(x, router_w, router_b, w_up, b_up, w_down, b_down):
    bsz, s, d = x.shape
    tokens = x.reshape(-1, d)
    n_tok = tokens.shape[0]
    logits = (tokens @ router_w + router_b).astype(jnp.float32)
    top_vals, top_idx = lax.top_k(logits, TOP_K)
    gates = jax.nn.softmax(top_vals, axis=-1).astype(x.dtype)
    n_assign = n_tok * TOP_K
    flat_exp = top_idx.reshape(-1).astype(jnp.int32)
    flat_tok = jnp.arange(n_assign, dtype=jnp.int32) // TOP_K
    flat_gate = gates.reshape(-1)
    order = jnp.argsort(flat_exp)
    sorted_exp = flat_exp[order]
    counts = jnp.zeros(N_EXPERTS, jnp.int32).at[flat_exp].add(1)
    padded_counts = (counts + MOE_BLOCK - 1) // MOE_BLOCK * MOE_BLOCK
    start = jnp.cumsum(counts) - counts
    padded_end = jnp.cumsum(padded_counts)
    padded_start = padded_end - padded_counts
    dest = padded_start[sorted_exp] + jnp.arange(n_assign, dtype=jnp.int32) - start[sorted_exp]
    n_blocks = -(-n_assign // MOE_BLOCK) + N_EXPERTS
    n_rows = n_blocks * MOE_BLOCK
    row_tok = jnp.zeros(n_rows, jnp.int32).at[dest].set(flat_tok[order])
    row_gate = jnp.zeros(n_rows, x.dtype).at[dest].set(flat_gate[order])
    block_start = jnp.arange(n_blocks, dtype=jnp.int32) * MOE_BLOCK
    block_exp = jnp.minimum(jnp.searchsorted(padded_end, block_start, side='right'),
                            N_EXPERTS - 1)
    xs = tokens[row_tok].reshape(n_blocks, MOE_BLOCK, d)

    def expert_block(args):
        xb, e = args
        hb = xb @ w_up[e] + b_up[e]
        g, lin = jnp.split(hb, 2, axis=-1)
        g = jnp.minimum(g, SWIGLU_LIMIT)
        lin = jnp.clip(lin, -SWIGLU_LIMIT, SWIGLU_LIMIT)
        act = g * jax.nn.sigmoid(SWIGLU_ALPHA * g) * (lin + 1.0)
        return act @ w_down[e] + b_down[e]

    ys = lax.map(expert_block, (xs, block_exp)).reshape(n_rows, d)
    out = jnp.zeros_like(tokens).at[row_tok].add(ys * row_gate[:, None])
    return out.reshape(bsz, s, d)


def setup_inputs(seed: int = 0) -> dict:
    key = jax.random.key(seed)
    ks = jax.random.split(key, 29)
    f32 = jnp.float32
    nrm = lambda k, shape, sc: jax.random.normal(k, shape, f32) * sc
    ones_n = lambda k, shape: 1.0 + 0.1 * jax.random.normal(k, shape, f32)
    return {
        "x": jax.random.normal(ks[0], (BATCH, SEQ, D_MODEL), f32),
        "ev_w_in": nrm(ks[1], (N_EVEN, D_MODEL, EVEN_IN), D_MODEL ** -0.5),
        "ev_b_in": nrm(ks[2], (N_EVEN, EVEN_IN), 0.02),
        "pool_w": nrm(ks[3], (N_EVEN, POOL_GROUPS, POOL_GROUP_DIM, POOL_GROUP_DIM), POOL_GROUP_DIM ** -0.5),
        "pool_b": nrm(ks[4], (N_EVEN, POOL_GROUPS, POOL_GROUP_DIM), 0.02),
        "pool_scale": ones_n(ks[5], (N_EVEN, POOL_WIDTH)),
        "conv_w": nrm(ks[6], (N_EVEN, CONV_KERNEL, CONV_WIDTH), CONV_KERNEL ** -0.5),
        "conv_b": nrm(ks[7], (N_EVEN, CONV_WIDTH), 0.02),
        "conv_ln_g": ones_n(ks[8], (N_EVEN, CONV_WIDTH)),
        "conv_ln_b": nrm(ks[9], (N_EVEN, CONV_WIDTH), 0.02),
        "ev_w_out": nrm(ks[10], (N_EVEN, EVEN_MIX, D_MODEL), EVEN_MIX ** -0.5 * DN_BETA),
        "ev_b_out": nrm(ks[11], (N_EVEN, D_MODEL), 0.02),
        "od_w_in": nrm(ks[12], (N_ODD, D_MODEL, 2 * SGU_WIDTH), D_MODEL ** -0.5),
        "od_b_in": nrm(ks[13], (N_ODD, 2 * SGU_WIDTH), 0.02),
        "sgu_ln_g": ones_n(ks[14], (N_ODD, SGU_WIDTH)),
        "sgu_ln_b": nrm(ks[15], (N_ODD, SGU_WIDTH), 0.02),
        "sgu_w": nrm(ks[16], (N_ODD, SGU_HEADS, SGU_CHUNK, SGU_CHUNK), SGU_CHUNK ** -0.5),
        "sgu_b": ones_n(ks[17], (N_ODD, SGU_HEADS, SGU_CHUNK)),
        "od_w_out": nrm(ks[18], (N_ODD, SGU_WIDTH, D_MODEL), SGU_WIDTH ** -0.5 * DN_BETA),
        "od_b_out": nrm(ks[19], (N_ODD, D_MODEL), 0.02),
        "router_w": nrm(ks[20], (DEPTH, D_MODEL, N_EXPERTS), D_MODEL ** -0.5),
        "router_b": nrm(ks[21], (DEPTH, N_EXPERTS), 0.01),
        "exp_w_up": nrm(ks[22], (DEPTH, N_EXPERTS, D_MODEL, 2 * D_EXPERT), D_MODEL ** -0.5),
        "exp_b_up": nrm(ks[23], (DEPTH, N_EXPERTS, 2 * D_EXPERT), 0.02),
        "exp_w_down": nrm(ks[24], (DEPTH, N_EXPERTS, D_EXPERT, D_MODEL), D_EXPERT ** -0.5 * DN_BETA),
        "exp_b_down": nrm(ks[25], (DEPTH, N_EXPERTS, D_MODEL), 0.02),
        "ln_g": ones_n(ks[26], (DEPTH, 2, D_MODEL)),
        "ln_b": nrm(ks[27], (DEPTH, 2, D_MODEL), 0.02),
    }


def reference(x, ev_w_in, ev_b_in, pool_w, pool_b, pool_scale, conv_w, conv_b,
              conv_ln_g, conv_ln_b, ev_w_out, ev_b_out, od_w_in, od_b_in,
              sgu_ln_g, sgu_ln_b, sgu_w, sgu_b, od_w_out, od_b_out,
              router_w, router_b, exp_w_up, exp_b_up, exp_w_down, exp_b_down,
              ln_g, ln_b):
    for layer in range(DEPTH):
        i = layer // 2
        if layer % 2 == 0:
            mix = even_mixer(x, ev_w_in[i], ev_b_in[i], pool_w[i], pool_b[i], pool_scale[i],
                             conv_w[i], conv_b[i], conv_ln_g[i], conv_ln_b[i],
                             ev_w_out[i], ev_b_out[i])
        else:
            mix = odd_mixer(x, od_w_in[i], od_b_in[i], sgu_ln_g[i], sgu_ln_b[i],
                            sgu_w[i], sgu_b[i], od_w_out[i], od_b_out[i])
        x = layer_norm(DN_ALPHA * x + mix, ln_g[layer, 0], ln_b[layer, 0])
        ff = moe(x, router_w[layer], router_b[layer], exp_w_up[layer], exp_b_up[layer],
                 exp_w_down[layer], exp_b_down[layer])
        x = layer_norm(DN_ALPHA * x + ff, ln_g[layer, 1], ln_b[layer, 1])
    return x
```

```python
import functools
import math

import jax
import jax.numpy as jnp
from jax import lax
from jax.experimental import pallas as pl
from jax.experimental.pallas import tpu as pltpu

F32 = jnp.float32
BF16 = jnp.bfloat16
I32 = jnp.int32

D_MODEL = 1024
DEPTH = 4
POOL_WINDOWS = (2, 4, 8, 16)
POOL_GROUPS = len(POOL_WINDOWS)
POOL_WIDTH = D_MODEL // 2
POOL_GROUP_DIM = POOL_WIDTH // POOL_GROUPS
CONV_WIDTH = D_MODEL // 2
CONV_KERNEL = 31
EVEN_IN = POOL_WIDTH + 2 * CONV_WIDTH
EVEN_MIX = POOL_WIDTH + CONV_WIDTH
SGU_WIDTH = 2 * D_MODEL
SGU_HEADS = 8
SGU_HEAD_DIM = SGU_WIDTH // SGU_HEADS
SGU_CHUNK = 128
STREAM_CHUNK = 64
N_EXPERTS = 32
TOP_K = 4
D_EXPERT = D_MODEL
SWIGLU_LIMIT = 7.0
SWIGLU_ALPHA = 1.702
DN_ALPHA = (2 * DEPTH) ** 0.25
LN_EPS = 1e-5

EVEN_TILE = 256
EVEN_HALO = 32
EVEN_ROW_CHUNK = 32
ODD_TILE = 256
ODD_COL_CHUNK = 512
ODD_LN_ROWS = 16
ROUTE_TILE = 512
DISPATCH_TILE = 512
EXPERT_BLOCK = 256
COMBINE_TILE = 256
VMEM_LIMIT = 56 * 1024 * 1024


def _layer_norm(x, g, b):
    mu = jnp.mean(x, axis=-1, keepdims=True)
    xc = x - mu
    var = jnp.mean(xc * xc, axis=-1, keepdims=True)
    return xc * lax.rsqrt(var + LN_EPS) * g + b


def _dot(a, b):
    return jnp.dot(a, b, preferred_element_type=F32)


def _even_kernel(x_ref, win_ref, bin_ref, pw_ref, pb_ref, ps_ref, cw_ref, cb_ref, cg_ref, cbb_ref,
                 wout_ref, bout_ref, lg_ref, lb_ref, o_ref, pbuf, gbuf, pooled, ybuf):
    j = pl.program_id(1)
    tile = x_ref.shape[0]
    halo = EVEN_HALO

    @pl.when(j == 0)
    def _():
        pbuf[0:halo, :] = jnp.zeros((halo, POOL_WIDTH), F32)
        gbuf[0:halo, :] = jnp.zeros((halo, CONV_WIDTH), F32)

    @pl.when(j > 0)
    def _():
        pbuf[0:halo, :] = pbuf[tile:tile + halo, :]
        gbuf[0:halo, :] = gbuf[tile:tile + halo, :]

    x = x_ref[...]
    h = _dot(x.astype(BF16), win_ref[...]) + bin_ref[...]
    pbuf[halo:halo + tile, :] = h[:, :POOL_WIDTH]
    a = h[:, POOL_WIDTH:POOL_WIDTH + CONV_WIDTH]
    gate = h[:, POOL_WIDTH + CONV_WIDTH:]
    gbuf[halo:halo + tile, :] = a * jax.nn.sigmoid(gate)

    rc = EVEN_ROW_CHUNK
    for r0 in range(0, tile, rc):
        pos = (j * tile + r0 + lax.broadcasted_iota(I32, (rc, 1), 0)).astype(F32)
        for g, win in enumerate(POOL_WINDOWS):
            c0 = g * POOL_GROUP_DIM
            cur = pbuf[halo + r0:halo + r0 + rc, c0:c0 + POOL_GROUP_DIM]
            wsum = cur
            for k in range(1, win):
                wsum = wsum + pbuf[halo + r0 - k:halo + r0 - k + rc, c0:c0 + POOL_GROUP_DIM]
            count = jnp.minimum(pos + 1.0, float(win))
            pooled[r0:r0 + rc, c0:c0 + POOL_GROUP_DIM] = (wsum / count - cur).astype(BF16)
        base = halo + r0 - (CONV_KERNEL - 1)
        acc = cw_ref[0:1, :] * gbuf[base:base + rc, :]
        for k in range(1, CONV_KERNEL):
            acc = acc + cw_ref[k:k + 1, :] * gbuf[base + k:base + k + rc, :]
        yc = _layer_norm(acc + cb_ref[...], cg_ref[...], cbb_ref[...])
        ybuf[r0:r0 + rc, POOL_WIDTH:] = (yc * jax.nn.sigmoid(yc)).astype(BF16)

    for g in range(POOL_GROUPS):
        c0 = g * POOL_GROUP_DIM
        yp = _dot(pooled[:, c0:c0 + POOL_GROUP_DIM], pw_ref[g]) + pb_ref[:, c0:c0 + POOL_GROUP_DIM]
        ybuf[:, c0:c0 + POOL_GROUP_DIM] = (yp * ps_ref[:, c0:c0 + POOL_GROUP_DIM]).astype(BF16)

    mix = _dot(ybuf[...], wout_ref[...]) + bout_ref[...]
    o_ref[...] = _layer_norm(DN_ALPHA * x + mix, lg_ref[...], lb_ref[...])


def _even_layer(x2d, batch, seq, w_in, b_in, pool_w, pool_b, pool_scale, conv_w, conv_b, conv_ln_g,
                conv_ln_b, w_out, b_out, ln_g, ln_b):
    tile = min(EVEN_TILE, seq)
    n_seq = seq // tile
    full = lambda shape: pl.BlockSpec(shape, lambda b, j: (0,) * len(shape))
    row = lambda v: v.reshape(1, -1)
    return pl.pallas_call(
        _even_kernel,
        grid=(batch, n_seq),
        in_specs=[
            pl.BlockSpec((tile, D_MODEL), lambda b, j: (b * n_seq + j, 0)),
            full((D_MODEL, EVEN_IN)), full((1, EVEN_IN)),
            full((POOL_GROUPS, POOL_GROUP_DIM, POOL_GROUP_DIM)), full((1, POOL_WIDTH)), full((1, POOL_WIDTH)),
            full((CONV_KERNEL, CONV_WIDTH)), full((1, CONV_WIDTH)), full((1, CONV_WIDTH)), full((1, CONV_WIDTH)),
            full((EVEN_MIX, D_MODEL)), full((1, D_MODEL)), full((1, D_MODEL)), full((1, D_MODEL)),
        ],
        out_specs=pl.BlockSpec((tile, D_MODEL), lambda b, j: (b * n_seq + j, 0)),
        out_shape=jax.ShapeDtypeStruct(x2d.shape, F32),
        scratch_shapes=[
            pltpu.VMEM((EVEN_HALO + tile, POOL_WIDTH), F32),
            pltpu.VMEM((EVEN_HALO + tile, CONV_WIDTH), F32),
            pltpu.VMEM((tile, POOL_WIDTH), BF16),
            pltpu.VMEM((tile, EVEN_MIX), BF16),
        ],
        compiler_params=pltpu.CompilerParams(
            dimension_semantics=("arbitrary", "arbitrary"), vmem_limit_bytes=VMEM_LIMIT),
        name="even_mixer",
    )(x2d, w_in.astype(BF16), row(b_in), pool_w.astype(BF16), row(pool_b), row(pool_scale), conv_w,
      row(conv_b), row(conv_ln_g), row(conv_ln_b), w_out.astype(BF16), row(b_out), row(ln_g), row(ln_b))


def _odd_kernel(x_ref, win_ref, bin_ref, sg_ref, sb_ref, ws_ref, bst_ref, wout_ref, bout_ref, lg_ref, lb_ref,
                o_ref, ubuf, vbuf, vnorm, gated):
    tile = x_ref.shape[0]
    x = x_ref[...]
    xb = x.astype(BF16)
    n_col = 2 * SGU_WIDTH // ODD_COL_CHUNK
    for c in range(n_col):
        c0 = c * ODD_COL_CHUNK
        hc = _dot(xb, win_ref[:, c0:c0 + ODD_COL_CHUNK]) + bin_ref[:, c0:c0 + ODD_COL_CHUNK]
        hc = 0.5 * hc * (1.0 + lax.erf(hc * (1.0 / math.sqrt(2.0))))
        if c0 < SGU_WIDTH:
            ubuf[:, c0:c0 + ODD_COL_CHUNK] = hc
        else:
            vbuf[:, c0 - SGU_WIDTH:c0 - SGU_WIDTH + ODD_COL_CHUNK] = hc

    def ln_rows(r, carry):
        r0 = pl.multiple_of(r * ODD_LN_ROWS, ODD_LN_ROWS)
        v = vbuf[pl.ds(r0, ODD_LN_ROWS), :]
        vnorm[pl.ds(r0, ODD_LN_ROWS), :] = _layer_norm(v, sg_ref[...], sb_ref[...]).astype(BF16)
        return carry

    lax.fori_loop(0, tile // ODD_LN_ROWS, ln_rows, 0)

    bi = lax.broadcasted_iota(I32, (SGU_CHUNK, SGU_CHUNK), 0) // STREAM_CHUNK
    bj = lax.broadcasted_iota(I32, (SGU_CHUNK, SGU_CHUNK), 1) // STREAM_CHUNK
    keep = bi >= bj
    for hd in range(SGU_HEADS):
        wm = jnp.where(keep, ws_ref[hd], 0.0).astype(BF16)
        c0 = hd * SGU_HEAD_DIM
        for n in range(tile // SGU_CHUNK):
            r0 = n * SGU_CHUNK
            sv = _dot(wm, vnorm[r0:r0 + SGU_CHUNK, c0:c0 + SGU_HEAD_DIM]) + bst_ref[:, hd:hd + 1]
            gated[r0:r0 + SGU_CHUNK, c0:c0 + SGU_HEAD_DIM] = (
                ubuf[r0:r0 + SGU_CHUNK, c0:c0 + SGU_HEAD_DIM] * sv).astype(BF16)

    mix = _dot(gated[...], wout_ref[...]) + bout_ref[...]
    o_ref[...] = _layer_norm(DN_ALPHA * x + mix, lg_ref[...], lb_ref[...])


def _odd_layer(x2d, w_in, b_in, sgu_ln_g, sgu_ln_b, sgu_w, sgu_b, w_out, b_out, ln_g, ln_b):
    n_tok = x2d.shape[0]
    tile = min(ODD_TILE, n_tok)
    full = lambda shape: pl.BlockSpec(shape, lambda i: (0,) * len(shape))
    row = lambda v: v.reshape(1, -1)
    return pl.pallas_call(
        _odd_kernel,
        grid=(n_tok // tile,),
        in_specs=[
            pl.BlockSpec((tile, D_MODEL), lambda i: (i, 0)),
            full((D_MODEL, 2 * SGU_WIDTH)), full((1, 2 * SGU_WIDTH)),
            full((1, SGU_WIDTH)), full((1, SGU_WIDTH)),
            full((SGU_HEADS, SGU_CHUNK, SGU_CHUNK)), full((SGU_CHUNK, SGU_HEADS)),
            full((SGU_WIDTH, D_MODEL)), full((1, D_MODEL)), full((1, D_MODEL)), full((1, D_MODEL)),
        ],
        out_specs=pl.BlockSpec((tile, D_MODEL), lambda i: (i, 0)),
        out_shape=jax.ShapeDtypeStruct(x2d.shape, F32),
        scratch_shapes=[
            pltpu.VMEM((tile, SGU_WIDTH), F32),
            pltpu.VMEM((tile, SGU_WIDTH), F32),
            pltpu.VMEM((tile, SGU_WIDTH), BF16),
            pltpu.VMEM((tile, SGU_WIDTH), BF16),
        ],
        compiler_params=pltpu.CompilerParams(
            dimension_semantics=("arbitrary",), vmem_limit_bytes=VMEM_LIMIT),
        name="odd_mixer",
    )(x2d, w_in.astype(BF16), row(b_in), row(sgu_ln_g), row(sgu_ln_b), sgu_w, sgu_b.T,
      w_out.astype(BF16), row(b_out), row(ln_g), row(ln_b))


def _route_kernel(x_ref, whi_ref, wlo_ref, rb_ref, gate_ref, dest_ref, bexp_ref, pad_ref,
                  idx_all, rank_all, cnt):
    i = pl.program_id(0)
    n_steps = pl.num_programs(0)
    tile = x_ref.shape[0]
    n_exp = N_EXPERTS

    @pl.when(i == 0)
    def _():
        cnt[...] = jnp.zeros_like(cnt)

    x = x_ref[...]
    x_hi = x.astype(BF16)
    x_lo = (x - x_hi.astype(F32)).astype(BF16)
    dn = (((1,), (1,)), ((), ()))
    nt = lambda w, v: lax.dot_general(w, v, dn, preferred_element_type=F32)
    logits = nt(whi_ref[...], x_hi) + nt(wlo_ref[...], x_hi) + nt(whi_ref[...], x_lo) + rb_ref[...]

    e_iota = lax.broadcasted_iota(I32, (n_exp, tile), 0)
    work = logits
    vals, sels, hits = [], [], []
    for _ in range(TOP_K):
        m = jnp.max(work, axis=0, keepdims=True)
        sel = jnp.min(jnp.where(work == m, e_iota, n_exp), axis=0, keepdims=True)
        hit = e_iota == sel
        vals.append(m)
        sels.append(sel)
        hits.append(hit)
        work = jnp.where(hit, -jnp.inf, work)

    exps = [jnp.exp(v - vals[0]) for v in vals]
    denom = exps[0] + exps[1] + exps[2] + exps[3]
    gates = jnp.concatenate([e / denom for e in exps], axis=0)
    col = pl.ds(pl.multiple_of(i * tile, tile), tile)
    gate_ref[:, col] = gates

    member = (hits[0] | hits[1] | hits[2] | hits[3]).astype(BF16)
    before = (lax.broadcasted_iota(I32, (tile, tile), 0) < lax.broadcasted_iota(I32, (tile, tile), 1)).astype(BF16)
    prior = _dot(member, before) + cnt[...]
    ranks = [jnp.sum(jnp.where(hit, prior, 0.0), axis=0, keepdims=True) for hit in hits]
    idx_all[:, col] = jnp.concatenate(sels, axis=0)
    rank_all[:, col] = jnp.concatenate(ranks, axis=0).astype(I32)
    cnt[...] = cnt[...] + jnp.sum(member.astype(F32), axis=1, keepdims=True)

    @pl.when(i == n_steps - 1)
    def _():
        counts = cnt[...]
        nblk = jnp.floor((counts + (EXPERT_BLOCK - 1.0)) * (1.0 / EXPERT_BLOCK))
        lower = (lax.broadcasted_iota(I32, (n_exp, n_exp), 1) < lax.broadcasted_iota(I32, (n_exp, n_exp), 0))
        blk_start = _dot(lower.astype(BF16), jnp.broadcast_to(nblk, (n_exp, 128)).astype(BF16))[:, 0:1]
        blk_end = blk_start + nblk
        row_start = (blk_start * EXPERT_BLOCK).astype(I32)
        idx = idx_all[...]
        dest = rank_all[...]
        for e in range(n_exp):
            dest = dest + jnp.where(idx == e, row_start[e:e + 1, :], 0)
        dest_ref[...] = dest
        b_iota = lax.broadcasted_iota(I32, (n_exp, bexp_ref.shape[1]), 1).astype(F32)
        owner = jnp.sum((b_iota >= blk_end).astype(I32), axis=0, keepdims=True)
        bexp_ref[...] = jnp.broadcast_to(jnp.minimum(owner, n_exp - 1), bexp_ref.shape)
        pad_lo = row_start + counts.astype(I32)
        pad_hi = (blk_end * EXPERT_BLOCK).astype(I32)
        lane = lax.broadcasted_iota(I32, (n_exp, 128), 1)
        total = jnp.max(blk_end, axis=0, keepdims=True).astype(I32)
        pad_ref[...] = jnp.where(lane == 0, pad_lo, jnp.where(lane == 1, pad_hi, total))


def _n_blocks(n_tok):
    return n_tok * TOP_K // EXPERT_BLOCK + N_EXPERTS


def _route(x2d, router_w, router_b):
    n_tok = x2d.shape[0]
    tile = min(ROUTE_TILE, n_tok)
    nb_pad = -(-_n_blocks(n_tok) // 128) * 128
    w_t = router_w.T
    w_hi = w_t.astype(BF16)
    w_lo = (w_t - w_hi.astype(F32)).astype(BF16)
    full = lambda shape: pl.BlockSpec(shape, lambda i: (0,) * len(shape))
    gate_t, dest_t, bexp, pad = pl.pallas_call(
        _route_kernel,
        grid=(n_tok // tile,),
        in_specs=[
            pl.BlockSpec((tile, D_MODEL), lambda i: (i, 0)),
            full((N_EXPERTS, D_MODEL)), full((N_EXPERTS, D_MODEL)), full((N_EXPERTS, 1)),
        ],
        out_specs=[full((TOP_K, n_tok)), full((TOP_K, n_tok)), full((8, nb_pad)), full((N_EXPERTS, 128))],
        out_shape=[
            jax.ShapeDtypeStruct((TOP_K, n_tok), F32),
            jax.ShapeDtypeStruct((TOP_K, n_tok), I32),
            jax.ShapeDtypeStruct((8, nb_pad), I32),
            jax.ShapeDtypeStruct((N_EXPERTS, 128), I32),
        ],
        scratch_shapes=[
            pltpu.VMEM((TOP_K, n_tok), I32),
            pltpu.VMEM((TOP_K, n_tok), I32),
            pltpu.VMEM((N_EXPERTS, 1), F32),
        ],
        compiler_params=pltpu.CompilerParams(
            dimension_semantics=("arbitrary",), vmem_limit_bytes=VMEM_LIMIT),
        name="route",
    )(x2d, w_hi, w_lo, router_b.reshape(N_EXPERTS, 1))
    return gate_t, dest_t, bexp, pad


def _dispatch_kernel(dest_ref, padlo_ref, padhi_ref, nused_ref, x_ref, xs_ref, zblk, sem):
    i = pl.program_id(0)
    tile = x_ref.shape[0]
    n_tok = tile * pl.num_programs(0)
    n_blocks = xs_ref.shape[0] // EXPERT_BLOCK

    def row_copy(src, t, d):
        return pltpu.make_async_copy(src.at[pl.ds(t, 1), :], xs_ref.at[pl.ds(d, 1), :], sem)

    def block_copy(b):
        r0 = pl.multiple_of(b * EXPERT_BLOCK, EXPERT_BLOCK)
        return pltpu.make_async_copy(zblk, xs_ref.at[pl.ds(r0, EXPERT_BLOCK), :], sem)

    @pl.when(i == 0)
    def _():
        zblk[...] = jnp.zeros_like(zblk)

        def per_expert(e, carry):
            lo = padlo_ref[e]
            hi = padhi_ref[e]

            def fill(r, c):
                row_copy(zblk, 0, r).start()
                return c

            lax.fori_loop(lo, hi, fill, 0)

            def drain(r, c):
                row_copy(zblk, 0, r).wait()
                return c

            lax.fori_loop(lo, hi, drain, 0)
            return carry

        lax.fori_loop(0, N_EXPERTS, per_expert, 0)

        def fill_block(b, c):
            block_copy(b).start()
            return c

        lax.fori_loop(nused_ref[0], n_blocks, fill_block, 0)

        def drain_block(b, c):
            block_copy(b).wait()
            return c

        lax.fori_loop(nused_ref[0], n_blocks, drain_block, 0)

    def issue(t, carry):
        for k in range(TOP_K):
            row_copy(x_ref, t, dest_ref[k * n_tok + i * tile + t]).start()
        return carry

    lax.fori_loop(0, tile, issue, 0)

    def drain_all(t, carry):
        for k in range(TOP_K):
            row_copy(x_ref, t, dest_ref[k * n_tok + i * tile + t]).wait()
        return carry

    lax.fori_loop(0, tile, drain_all, 0)


def _dispatch(x2d, dest_flat, pad_lo, pad_hi, nused):
    n_tok = x2d.shape[0]
    tile = min(DISPATCH_TILE, n_tok)
    n_rows = _n_blocks(n_tok) * EXPERT_BLOCK
    return pl.pallas_call(
        _dispatch_kernel,
        grid_spec=pltpu.PrefetchScalarGridSpec(
            num_scalar_prefetch=4,
            grid=(n_tok // tile,),
            in_specs=[pl.BlockSpec((tile, D_MODEL), lambda i, *_: (i, 0))],
            out_specs=pl.BlockSpec(memory_space=pl.ANY),
            scratch_shapes=[pltpu.VMEM((EXPERT_BLOCK, D_MODEL), F32), pltpu.SemaphoreType.DMA],
        ),
        out_shape=jax.ShapeDtypeStruct((n_rows, D_MODEL), F32),
        compiler_params=pltpu.CompilerParams(
            dimension_semantics=("arbitrary",), vmem_limit_bytes=VMEM_LIMIT),
        name="dispatch",
    )(dest_flat, pad_lo, pad_hi, nused, x2d)


def _expert_kernel(bexp_ref, nused_ref, x_ref, wup_ref, bup_ref, wdn_ref, bdn_ref, o_ref, wup_bf, wdn_bf):
    i = pl.program_id(0)
    prev = bexp_ref[jnp.maximum(i - 1, 0)]
    changed = jnp.logical_or(i == 0, bexp_ref[i] != prev)

    @pl.when(jnp.logical_and(changed, i < nused_ref[0]))
    def _():
        wup_bf[...] = wup_ref[...].astype(BF16)
        wdn_bf[...] = wdn_ref[...].astype(BF16)

    @pl.when(i < nused_ref[0])
    def _():
        xb = x_ref[...].astype(BF16)
        hb = _dot(xb, wup_bf[...]) + bup_ref[...]
        g = jnp.minimum(hb[:, :D_EXPERT], SWIGLU_LIMIT)
        lin = jnp.clip(hb[:, D_EXPERT:], -SWIGLU_LIMIT, SWIGLU_LIMIT)
        act = g * jax.nn.sigmoid(SWIGLU_ALPHA * g) * (lin + 1.0)
        o_ref[...] = _dot(act.astype(BF16), wdn_bf[...]) + bdn_ref[...]

    @pl.when(i >= nused_ref[0])
    def _():
        o_ref[...] = jnp.zeros_like(o_ref)


def _experts(xs, bexp, nused, layer, w_up, b_up, w_down, b_down):
    n_rows = xs.shape[0]
    n_blocks = n_rows // EXPERT_BLOCK
    blk = lambda i, be, nu: (jnp.minimum(i, nu[0] - 1), 0)
    out_blk = lambda i, be, nu: (i, 0)
    wsel = lambda i, be, nu: (layer, be[jnp.minimum(i, nu[0] - 1)], 0, 0)
    return pl.pallas_call(
        _expert_kernel,
        grid_spec=pltpu.PrefetchScalarGridSpec(
            num_scalar_prefetch=2,
            grid=(n_blocks,),
            in_specs=[
                pl.BlockSpec((EXPERT_BLOCK, D_MODEL), blk),
                pl.BlockSpec((None, None, D_MODEL, 2 * D_EXPERT), wsel),
                pl.BlockSpec((None, None, 1, 2 * D_EXPERT), wsel),
                pl.BlockSpec((None, None, D_EXPERT, D_MODEL), wsel),
                pl.BlockSpec((None, None, 1, D_MODEL), wsel),
            ],
            out_specs=pl.BlockSpec((EXPERT_BLOCK, D_MODEL), out_blk),
            scratch_shapes=[pltpu.VMEM((D_MODEL, 2 * D_EXPERT), BF16), pltpu.VMEM((D_EXPERT, D_MODEL), BF16)],
        ),
        out_shape=jax.ShapeDtypeStruct((n_rows, D_MODEL), F32),
        compiler_params=pltpu.CompilerParams(
            dimension_semantics=("arbitrary",), vmem_limit_bytes=VMEM_LIMIT),
        name="experts",
    )(bexp, nused, xs, w_up, b_up.reshape(DEPTH, N_EXPERTS, 1, 2 * D_EXPERT), w_down,
      b_down.reshape(DEPTH, N_EXPERTS, 1, D_MODEL))


def _combine_kernel(dest_ref, x_ref, gate_ref, lg_ref, lb_ref, ys_ref, o_ref, ybuf, sem):
    i = pl.program_id(0)
    tile = x_ref.shape[0]
    n_tok = tile * pl.num_programs(0)

    def row_copy(t, k):
        d = dest_ref[k * n_tok + i * tile + t]
        return pltpu.make_async_copy(ys_ref.at[pl.ds(d, 1), :], ybuf.at[k, pl.ds(t, 1), :], sem)

    def issue(t, carry):
        for k in range(TOP_K):
            row_copy(t, k).start()
        return carry

    lax.fori_loop(0, tile, issue, 0)

    def drain(t, carry):
        for k in range(TOP_K):
            row_copy(t, k).wait()
        return carry

    lax.fori_loop(0, tile, drain, 0)

    ff = gate_ref[:, 0:1] * ybuf[0]
    for k in range(1, TOP_K):
        ff = ff + gate_ref[:, k:k + 1] * ybuf[k]
    o_ref[...] = _layer_norm(DN_ALPHA * x_ref[...] + ff, lg_ref[...], lb_ref[...])


def _combine(x2d, ys, dest_flat, gates, ln_g, ln_b):
    n_tok = x2d.shape[0]
    tile = min(COMBINE_TILE, n_tok)
    return pl.pallas_call(
        _combine_kernel,
        grid_spec=pltpu.PrefetchScalarGridSpec(
            num_scalar_prefetch=1,
            grid=(n_tok // tile,),
            in_specs=[
                pl.BlockSpec((tile, D_MODEL), lambda i, *_: (i, 0)),
                pl.BlockSpec((tile, TOP_K), lambda i, *_: (i, 0)),
                pl.BlockSpec((1, D_MODEL), lambda i, *_: (0, 0)),
                pl.BlockSpec((1, D_MODEL), lambda i, *_: (0, 0)),
                pl.BlockSpec(memory_space=pl.ANY),
            ],
            out_specs=pl.BlockSpec((tile, D_MODEL), lambda i, *_: (i, 0)),
            scratch_shapes=[pltpu.VMEM((TOP_K, tile, D_MODEL), F32), pltpu.SemaphoreType.DMA],
        ),
        out_shape=jax.ShapeDtypeStruct(x2d.shape, F32),
        compiler_params=pltpu.CompilerParams(
            dimension_semantics=("arbitrary",), vmem_limit_bytes=VMEM_LIMIT),
        name="combine",
    )(dest_flat, x2d, gates, ln_g.reshape(1, -1), ln_b.reshape(1, -1), ys)


def _moe_layer(x2d, layer, router_w, router_b, w_up, b_up, w_down, b_down, ln_g, ln_b):
    n_blocks = _n_blocks(x2d.shape[0])
    gate_t, dest_t, bexp, pad = _route(x2d, router_w, router_b)
    dest_flat = dest_t.reshape(-1)
    nused = pad[0, 2:3]
    xs = _dispatch(x2d, dest_flat, pad[:, 0], pad[:, 1], nused)
    ys = _experts(xs, bexp[0, :n_blocks], nused, layer, w_up, b_up, w_down, b_down)
    return _combine(x2d, ys, dest_flat, gate_t.T, ln_g, ln_b)


def kernel(x, ev_w_in, ev_b_in, pool_w, pool_b, pool_scale, conv_w, conv_b, conv_ln_g, conv_ln_b, ev_w_out,
           ev_b_out, od_w_in, od_b_in, sgu_ln_g, sgu_ln_b, sgu_w, sgu_b, od_w_out, od_b_out, router_w, router_b,
           exp_w_up, exp_b_up, exp_w_down, exp_b_down, ln_g, ln_b):
    batch, seq, d = x.shape
    h = x.reshape(batch * seq, d)
    for layer in range(DEPTH):
        i = layer // 2
        if layer % 2 == 0:
            h = _even_layer(h, batch, seq, ev_w_in[i], ev_b_in[i], pool_w[i], pool_b[i], pool_scale[i],
                            conv_w[i], conv_b[i], conv_ln_g[i], conv_ln_b[i], ev_w_out[i], ev_b_out[i],
                            ln_g[layer, 0], ln_b[layer, 0])
        else:
            h = _odd_layer(h, od_w_in[i], od_b_in[i], sgu_ln_g[i], sgu_ln_b[i], sgu_w[i], sgu_b[i],
                           od_w_out[i], od_b_out[i], ln_g[layer, 0], ln_b[layer, 0])
        h = _moe_layer(h, layer, router_w[layer], router_b[layer], exp_w_up, exp_b_up, exp_w_down,
                       exp_b_down, ln_g[layer, 1], ln_b[layer, 1])
    return h.reshape(batch, seq, d)
```

```python
import functools
import math

import jax
import jax.numpy as jnp
from jax import lax
from jax.experimental import pallas as pl
from jax.experimental.pallas import tpu as pltpu

F32 = jnp.float32
BF16 = jnp.bfloat16
I32 = jnp.int32

D_MODEL = 1024
DEPTH = 4
POOL_WINDOWS = (2, 4, 8, 16)
POOL_GROUPS = len(POOL_WINDOWS)
POOL_WIDTH = D_MODEL // 2
POOL_GROUP_DIM = POOL_WIDTH // POOL_GROUPS
CONV_WIDTH = D_MODEL // 2
CONV_KERNEL = 31
EVEN_IN = POOL_WIDTH + 2 * CONV_WIDTH
EVEN_MIX = POOL_WIDTH + CONV_WIDTH
SGU_WIDTH = 2 * D_MODEL
SGU_HEADS = 8
SGU_HEAD_DIM = SGU_WIDTH // SGU_HEADS
SGU_CHUNK = 128
STREAM_CHUNK = 64
N_EXPERTS = 32
TOP_K = 4
D_EXPERT = D_MODEL
SWIGLU_LIMIT = 7.0
SWIGLU_ALPHA = 1.702
DN_ALPHA = (2 * DEPTH) ** 0.25
LN_EPS = 1e-5

EVEN_TILE = 256
EVEN_HALO = 32
EVEN_ROW_CHUNK = 32
ODD_TILE = 256
ODD_COL_CHUNK = 512
ODD_LN_ROWS = 32
ROUTE_TILE = 512
DISPATCH_TILE = 512
EXPERT_BLOCK = 256
COMBINE_TILE = 256
ROW_DMA_UNROLL = 8
VMEM_LIMIT = 56 * 1024 * 1024

LAYOUT_PAD_LO, LAYOUT_PAD_HI, LAYOUT_TOTAL_BLOCKS, LAYOUT_FIRST_BLOCK, LAYOUT_NUM_BLOCKS = range(5)


def _layer_norm(x, g, b):
    mu = jnp.mean(x, axis=-1, keepdims=True)
    xc = x - mu
    var = jnp.mean(xc * xc, axis=-1, keepdims=True)
    return xc * lax.rsqrt(var + LN_EPS) * g + b


def _dot(a, b):
    return jnp.dot(a, b, preferred_element_type=F32)


def _even_kernel(x_ref, win_ref, bin_ref, pw_ref, pb_ref, ps_ref, cw_ref, cb_ref, cg_ref, cbb_ref,
                 wout_ref, bout_ref, lg_ref, lb_ref, o_ref, pbuf, gbuf, pooled, ybuf):
    j = pl.program_id(1)
    tile = x_ref.shape[0]
    halo = EVEN_HALO

    @pl.when(j == 0)
    def _():
        pbuf[0:halo, :] = jnp.zeros((halo, POOL_WIDTH), F32)
        gbuf[0:halo, :] = jnp.zeros((halo, CONV_WIDTH), F32)

    @pl.when(j > 0)
    def _():
        pbuf[0:halo, :] = pbuf[tile:tile + halo, :]
        gbuf[0:halo, :] = gbuf[tile:tile + halo, :]

    x = x_ref[...]
    h = _dot(x.astype(BF16), win_ref[...]) + bin_ref[...]
    pbuf[halo:halo + tile, :] = h[:, :POOL_WIDTH]
    a = h[:, POOL_WIDTH:POOL_WIDTH + CONV_WIDTH]
    gate = h[:, POOL_WIDTH + CONV_WIDTH:]
    gbuf[halo:halo + tile, :] = a * jax.nn.sigmoid(gate)

    rc = EVEN_ROW_CHUNK
    for r0 in range(0, tile, rc):
        pos = (j * tile + r0 + lax.broadcasted_iota(I32, (rc, 1), 0)).astype(F32)
        for g, win in enumerate(POOL_WINDOWS):
            c0 = g * POOL_GROUP_DIM
            cur = pbuf[halo + r0:halo + r0 + rc, c0:c0 + POOL_GROUP_DIM]
            wsum = cur
            for k in range(1, win):
                wsum = wsum + pbuf[halo + r0 - k:halo + r0 - k + rc, c0:c0 + POOL_GROUP_DIM]
            count = jnp.minimum(pos + 1.0, float(win))
            pooled[r0:r0 + rc, c0:c0 + POOL_GROUP_DIM] = (wsum / count - cur).astype(BF16)
        base = halo + r0 - (CONV_KERNEL - 1)
        acc = cw_ref[0:1, :] * gbuf[base:base + rc, :]
        for k in range(1, CONV_KERNEL):
            acc = acc + cw_ref[k:k + 1, :] * gbuf[base + k:base + k + rc, :]
        yc = _layer_norm(acc + cb_ref[...], cg_ref[...], cbb_ref[...])
        ybuf[r0:r0 + rc, POOL_WIDTH:] = (yc * jax.nn.sigmoid(yc)).astype(BF16)

    for g in range(POOL_GROUPS):
        c0 = g * POOL_GROUP_DIM
        yp = _dot(pooled[:, c0:c0 + POOL_GROUP_DIM], pw_ref[g]) + pb_ref[:, c0:c0 + POOL_GROUP_DIM]
        ybuf[:, c0:c0 + POOL_GROUP_DIM] = (yp * ps_ref[:, c0:c0 + POOL_GROUP_DIM]).astype(BF16)

    mix = _dot(ybuf[...], wout_ref[...]) + bout_ref[...]
    o_ref[...] = _layer_norm(DN_ALPHA * x + mix, lg_ref[...], lb_ref[...])


def _even_layer(x2d, batch, seq, w_in, b_in, pool_w, pool_b, pool_scale, conv_w, conv_b, conv_ln_g,
                conv_ln_b, w_out, b_out, ln_g, ln_b):
    tile = min(EVEN_TILE, seq)
    n_seq = seq // tile
    full = lambda shape: pl.BlockSpec(shape, lambda b, j: (0,) * len(shape))
    row = lambda v: v.reshape(1, -1)
    return pl.pallas_call(
        _even_kernel,
        grid=(batch, n_seq),
        in_specs=[
            pl.BlockSpec((tile, D_MODEL), lambda b, j: (b * n_seq + j, 0)),
            full((D_MODEL, EVEN_IN)), full((1, EVEN_IN)),
            full((POOL_GROUPS, POOL_GROUP_DIM, POOL_GROUP_DIM)), full((1, POOL_WIDTH)), full((1, POOL_WIDTH)),
            full((CONV_KERNEL, CONV_WIDTH)), full((1, CONV_WIDTH)), full((1, CONV_WIDTH)), full((1, CONV_WIDTH)),
            full((EVEN_MIX, D_MODEL)), full((1, D_MODEL)), full((1, D_MODEL)), full((1, D_MODEL)),
        ],
        out_specs=pl.BlockSpec((tile, D_MODEL), lambda b, j: (b * n_seq + j, 0)),
        out_shape=jax.ShapeDtypeStruct(x2d.shape, F32),
        scratch_shapes=[
            pltpu.VMEM((EVEN_HALO + tile, POOL_WIDTH), F32),
            pltpu.VMEM((EVEN_HALO + tile, CONV_WIDTH), F32),
            pltpu.VMEM((tile, POOL_WIDTH), BF16),
            pltpu.VMEM((tile, EVEN_MIX), BF16),
        ],
        compiler_params=pltpu.CompilerParams(
            dimension_semantics=("arbitrary", "arbitrary"), vmem_limit_bytes=VMEM_LIMIT),
        name="even_mixer",
    )(x2d, w_in.astype(BF16), row(b_in), pool_w.astype(BF16), row(pool_b), row(pool_scale), conv_w,
      row(conv_b), row(conv_ln_g), row(conv_ln_b), w_out.astype(BF16), row(b_out), row(ln_g), row(ln_b))


def _odd_kernel(x_ref, win_ref, bin_ref, sg_ref, sb_ref, ws_ref, bst_ref, wout_ref, bout_ref, lg_ref, lb_ref,
                o_ref, ubuf, vbuf, vnorm, gated):
    tile = x_ref.shape[0]
    x = x_ref[...]
    xb = x.astype(BF16)
    n_col = 2 * SGU_WIDTH // ODD_COL_CHUNK
    for c in range(n_col):
        c0 = c * ODD_COL_CHUNK
        hc = _dot(xb, win_ref[:, c0:c0 + ODD_COL_CHUNK]) + bin_ref[:, c0:c0 + ODD_COL_CHUNK]
        hc = 0.5 * hc * (1.0 + lax.erf(hc * (1.0 / math.sqrt(2.0))))
        if c0 < SGU_WIDTH:
            ubuf[:, c0:c0 + ODD_COL_CHUNK] = hc
        else:
            vbuf[:, c0 - SGU_WIDTH:c0 - SGU_WIDTH + ODD_COL_CHUNK] = hc

    for r0 in range(0, tile, ODD_LN_ROWS):
        v = vbuf[r0:r0 + ODD_LN_ROWS, :]
        vnorm[r0:r0 + ODD_LN_ROWS, :] = _layer_norm(v, sg_ref[...], sb_ref[...]).astype(BF16)

    bi = lax.broadcasted_iota(I32, (SGU_CHUNK, SGU_CHUNK), 0) // STREAM_CHUNK
    bj = lax.broadcasted_iota(I32, (SGU_CHUNK, SGU_CHUNK), 1) // STREAM_CHUNK
    keep = bi >= bj
    for hd in range(SGU_HEADS):
        wm = jnp.where(keep, ws_ref[hd], 0.0).astype(BF16)
        c0 = hd * SGU_HEAD_DIM
        for n in range(tile // SGU_CHUNK):
            r0 = n * SGU_CHUNK
            sv = _dot(wm, vnorm[r0:r0 + SGU_CHUNK, c0:c0 + SGU_HEAD_DIM]) + bst_ref[:, hd:hd + 1]
            gated[r0:r0 + SGU_CHUNK, c0:c0 + SGU_HEAD_DIM] = (
                ubuf[r0:r0 + SGU_CHUNK, c0:c0 + SGU_HEAD_DIM] * sv).astype(BF16)

    mix = _dot(gated[...], wout_ref[...]) + bout_ref[...]
    o_ref[...] = _layer_norm(DN_ALPHA * x + mix, lg_ref[...], lb_ref[...])


def _odd_layer(x2d, w_in, b_in, sgu_ln_g, sgu_ln_b, sgu_w, sgu_b, w_out, b_out, ln_g, ln_b):
    n_tok = x2d.shape[0]
    tile = min(ODD_TILE, n_tok)
    full = lambda shape: pl.BlockSpec(shape, lambda i: (0,) * len(shape))
    row = lambda v: v.reshape(1, -1)
    return pl.pallas_call(
        _odd_kernel,
        grid=(n_tok // tile,),
        in_specs=[
            pl.BlockSpec((tile, D_MODEL), lambda i: (i, 0)),
            full((D_MODEL, 2 * SGU_WIDTH)), full((1, 2 * SGU_WIDTH)),
            full((1, SGU_WIDTH)), full((1, SGU_WIDTH)),
            full((SGU_HEADS, SGU_CHUNK, SGU_CHUNK)), full((SGU_CHUNK, SGU_HEADS)),
            full((SGU_WIDTH, D_MODEL)), full((1, D_MODEL)), full((1, D_MODEL)), full((1, D_MODEL)),
        ],
        out_specs=pl.BlockSpec((tile, D_MODEL), lambda i: (i, 0)),
        out_shape=jax.ShapeDtypeStruct(x2d.shape, F32),
        scratch_shapes=[
            pltpu.VMEM((tile, SGU_WIDTH), F32),
            pltpu.VMEM((tile, SGU_WIDTH), F32),
            pltpu.VMEM((tile, SGU_WIDTH), BF16),
            pltpu.VMEM((tile, SGU_WIDTH), BF16),
        ],
        compiler_params=pltpu.CompilerParams(
            dimension_semantics=("arbitrary",), vmem_limit_bytes=VMEM_LIMIT),
        name="odd_mixer",
    )(x2d, w_in.astype(BF16), row(b_in), row(sgu_ln_g), row(sgu_ln_b), sgu_w, sgu_b.T,
      w_out.astype(BF16), row(b_out), row(ln_g), row(ln_b))


def _route_kernel(x_ref, whi_ref, wlo_ref, rb_ref, gate_ref, dest_ref, layout_ref, idx_all, rank_all, cnt):
    i = pl.program_id(0)
    n_steps = pl.num_programs(0)
    tile = x_ref.shape[0]
    n_exp = N_EXPERTS

    @pl.when(i == 0)
    def _():
        cnt[...] = jnp.zeros_like(cnt)

    x = x_ref[...]
    x_hi = x.astype(BF16)
    x_lo = (x - x_hi.astype(F32)).astype(BF16)
    dn = (((1,), (1,)), ((), ()))
    nt = lambda w, v: lax.dot_general(w, v, dn, preferred_element_type=F32)
    logits = nt(whi_ref[...], x_hi) + nt(wlo_ref[...], x_hi) + nt(whi_ref[...], x_lo) + rb_ref[...]

    e_iota = lax.broadcasted_iota(I32, (n_exp, tile), 0)
    work = logits
    vals, sels, hits = [], [], []
    for _ in range(TOP_K):
        m = jnp.max(work, axis=0, keepdims=True)
        sel = jnp.min(jnp.where(work == m, e_iota, n_exp), axis=0, keepdims=True)
        hit = e_iota == sel
        vals.append(m)
        sels.append(sel)
        hits.append(hit)
        work = jnp.where(hit, -jnp.inf, work)

    exps = [jnp.exp(v - vals[0]) for v in vals]
    denom = exps[0] + exps[1] + exps[2] + exps[3]
    gates = jnp.concatenate([e / denom for e in exps], axis=0)
    col = pl.ds(pl.multiple_of(i * tile, tile), tile)
    gate_ref[:, col] = gates

    member = (hits[0] | hits[1] | hits[2] | hits[3]).astype(BF16)
    before = (lax.broadcasted_iota(I32, (tile, tile), 0) < lax.broadcasted_iota(I32, (tile, tile), 1)).astype(BF16)
    prior = _dot(member, before) + cnt[...]
    ranks = [jnp.sum(jnp.where(hit, prior, 0.0), axis=0, keepdims=True) for hit in hits]
    idx_all[:, col] = jnp.concatenate(sels, axis=0)
    rank_all[:, col] = jnp.concatenate(ranks, axis=0).astype(I32)
    cnt[...] = cnt[...] + jnp.sum(member.astype(F32), axis=1, keepdims=True)

    @pl.when(i == n_steps - 1)
    def _():
        counts = cnt[...]
        nblk = jnp.floor((counts + (EXPERT_BLOCK - 1.0)) * (1.0 / EXPERT_BLOCK))
        lower = (lax.broadcasted_iota(I32, (n_exp, n_exp), 1) < lax.broadcasted_iota(I32, (n_exp, n_exp), 0))
        blk_start = _dot(lower.astype(BF16), jnp.broadcast_to(nblk, (n_exp, 128)).astype(BF16))[:, 0:1]
        blk_end = blk_start + nblk
        row_start = (blk_start * EXPERT_BLOCK).astype(I32)
        idx = idx_all[...]
        dest = rank_all[...]
        for e in range(n_exp):
            dest = dest + jnp.where(idx == e, row_start[e:e + 1, :], 0)
        dest_ref[...] = dest
        pad_lo = row_start + counts.astype(I32)
        pad_hi = (blk_end * EXPERT_BLOCK).astype(I32)
        lane = lax.broadcasted_iota(I32, (n_exp, 128), 1)
        total = jnp.max(blk_end, axis=0, keepdims=True).astype(I32)
        table = jnp.where(lane == LAYOUT_PAD_LO, pad_lo, 0)
        table = jnp.where(lane == LAYOUT_PAD_HI, pad_hi, table)
        table = jnp.where(lane == LAYOUT_TOTAL_BLOCKS, total, table)
        table = jnp.where(lane == LAYOUT_FIRST_BLOCK, blk_start.astype(I32), table)
        table = jnp.where(lane == LAYOUT_NUM_BLOCKS, nblk.astype(I32), table)
        layout_ref[...] = table


def _n_blocks(n_tok):
    return n_tok * TOP_K // EXPERT_BLOCK + N_EXPERTS


def _route(x2d, router_w, router_b):
    n_tok = x2d.shape[0]
    tile = min(ROUTE_TILE, n_tok)
    w_t = router_w.T
    w_hi = w_t.astype(BF16)
    w_lo = (w_t - w_hi.astype(F32)).astype(BF16)
    full = lambda shape: pl.BlockSpec(shape, lambda i: (0,) * len(shape))
    return pl.pallas_call(
        _route_kernel,
        grid=(n_tok // tile,),
        in_specs=[
            pl.BlockSpec((tile, D_MODEL), lambda i: (i, 0)),
            full((N_EXPERTS, D_MODEL)), full((N_EXPERTS, D_MODEL)), full((N_EXPERTS, 1)),
        ],
        out_specs=[full((TOP_K, n_tok)), full((TOP_K, n_tok)), full((N_EXPERTS, 128))],
        out_shape=[
            jax.ShapeDtypeStruct((TOP_K, n_tok), F32),
            jax.ShapeDtypeStruct((TOP_K, n_tok), I32),
            jax.ShapeDtypeStruct((N_EXPERTS, 128), I32),
        ],
        scratch_shapes=[
            pltpu.VMEM((TOP_K, n_tok), I32),
            pltpu.VMEM((TOP_K, n_tok), I32),
            pltpu.VMEM((N_EXPERTS, 1), F32),
        ],
        compiler_params=pltpu.CompilerParams(
            dimension_semantics=("arbitrary",), vmem_limit_bytes=VMEM_LIMIT),
        name="route",
    )(x2d, w_hi, w_lo, router_b.reshape(N_EXPERTS, 1))


def _dispatch_kernel(dest_ref, padlo_ref, padhi_ref, nused_ref, x_ref, xs_ref, zblk, sem):
    i = pl.program_id(0)
    tile = x_ref.shape[0]
    n_tok = tile * pl.num_programs(0)
    n_blocks = xs_ref.shape[0] // EXPERT_BLOCK

    def row_copy(src, t, d):
        return pltpu.make_async_copy(src.at[pl.ds(t, 1), :], xs_ref.at[pl.ds(d, 1), :], sem)

    def block_copy(b):
        r0 = pl.multiple_of(b * EXPERT_BLOCK, EXPERT_BLOCK)
        return pltpu.make_async_copy(zblk, xs_ref.at[pl.ds(r0, EXPERT_BLOCK), :], sem)

    @pl.when(i == 0)
    def _():
        zblk[...] = jnp.zeros_like(zblk)

        def per_expert(e, carry):
            lo = padlo_ref[e]
            hi = padhi_ref[e]

            def fill(r, c):
                row_copy(zblk, 0, r).start()
                return c

            lax.fori_loop(lo, hi, fill, 0)

            def drain(r, c):
                row_copy(zblk, 0, r).wait()
                return c

            lax.fori_loop(lo, hi, drain, 0)
            return carry

        lax.fori_loop(0, N_EXPERTS, per_expert, 0)

        def fill_block(b, c):
            block_copy(b).start()
            return c

        lax.fori_loop(nused_ref[0], n_blocks, fill_block, 0)

        def drain_block(b, c):
            block_copy(b).wait()
            return c

        lax.fori_loop(nused_ref[0], n_blocks, drain_block, 0)

    def issue(step, carry):
        for u in range(ROW_DMA_UNROLL):
            t = step * ROW_DMA_UNROLL + u
            for k in range(TOP_K):
                row_copy(x_ref, t, dest_ref[k * n_tok + i * tile + t]).start(priority=k % 2)
        return carry

    lax.fori_loop(0, tile // ROW_DMA_UNROLL, issue, 0)
    for k in range(TOP_K):
        pltpu.make_async_copy(x_ref, xs_ref.at[pl.ds(0, tile), :], sem).wait()


def _dispatch(x2d, dest_flat, pad_lo, pad_hi, nused):
    n_tok = x2d.shape[0]
    tile = min(DISPATCH_TILE, n_tok)
    n_rows = _n_blocks(n_tok) * EXPERT_BLOCK
    return pl.pallas_call(
        _dispatch_kernel,
        grid_spec=pltpu.PrefetchScalarGridSpec(
            num_scalar_prefetch=4,
            grid=(n_tok // tile,),
            in_specs=[pl.BlockSpec((tile, D_MODEL), lambda i, *_: (i, 0))],
            out_specs=pl.BlockSpec(memory_space=pl.ANY),
            scratch_shapes=[pltpu.VMEM((EXPERT_BLOCK, D_MODEL), F32), pltpu.SemaphoreType.DMA],
        ),
        out_shape=jax.ShapeDtypeStruct((n_rows, D_MODEL), F32),
        compiler_params=pltpu.CompilerParams(
            dimension_semantics=("arbitrary",), vmem_limit_bytes=VMEM_LIMIT),
        name="dispatch",
    )(dest_flat, pad_lo, pad_hi, nused, x2d)


def _expert_kernel(first_ref, nblk_ref, nused_ref, xs_ref, wup_ref, bup_ref, wdn_ref, bdn_ref, ys_ref,
                   wup_bf, wdn_bf, xbuf, obuf, in_sem, out_sem):
    e = pl.program_id(0)
    nb = nblk_ref[e]
    b0 = first_ref[e]
    n_blocks = ys_ref.shape[0] // EXPERT_BLOCK

    def rows(b):
        return pl.ds(pl.multiple_of(b * EXPERT_BLOCK, EXPERT_BLOCK), EXPERT_BLOCK)

    def in_copy(b, slot):
        return pltpu.make_async_copy(xs_ref.at[rows(b), :], xbuf.at[slot], in_sem.at[slot])

    def out_copy(b, slot):
        return pltpu.make_async_copy(obuf.at[slot], ys_ref.at[rows(b), :], out_sem.at[slot])

    @pl.when(nb > 0)
    def _():
        in_copy(b0, 0).start()
        wup_bf[...] = wup_ref[...].astype(BF16)
        wdn_bf[...] = wdn_ref[...].astype(BF16)

        def block(j, carry):
            slot = lax.rem(j, 2)

            @pl.when(j + 1 < nb)
            def _():
                in_copy(b0 + j + 1, 1 - slot).start()

            in_copy(b0 + j, slot).wait()

            @pl.when(j >= 2)
            def _():
                out_copy(b0 + j - 2, slot).wait()

            xb = xbuf[slot].astype(BF16)
            hb = _dot(xb, wup_bf[...]) + bup_ref[...]
            g = jnp.minimum(hb[:, :D_EXPERT], SWIGLU_LIMIT)
            lin = jnp.clip(hb[:, D_EXPERT:], -SWIGLU_LIMIT, SWIGLU_LIMIT)
            act = g * jax.nn.sigmoid(SWIGLU_ALPHA * g) * (lin + 1.0)
            obuf[slot] = _dot(act.astype(BF16), wdn_bf[...]) + bdn_ref[...]
            out_copy(b0 + j, slot).start()
            return carry

        lax.fori_loop(0, nb, block, 0)

        @pl.when(nb >= 2)
        def _():
            out_copy(b0 + nb - 2, lax.rem(nb, 2)).wait()

        out_copy(b0 + nb - 1, lax.rem(nb + 1, 2)).wait()

    @pl.when(e == pl.num_programs(0) - 1)
    def _():
        obuf[0] = jnp.zeros(obuf.shape[1:], F32)

        def fill(b, carry):
            out_copy(b, 0).start()
            return carry

        lax.fori_loop(nused_ref[0], n_blocks, fill, 0)

        def drain(b, carry):
            out_copy(b, 0).wait()
            return carry

        lax.fori_loop(nused_ref[0], n_blocks, drain, 0)


def _experts(xs, first_block, num_blocks, nused, layer, w_up, b_up, w_down, b_down):
    n_rows = xs.shape[0]
    wsel = lambda e, *_: (layer, e, 0, 0)
    return pl.pallas_call(
        _expert_kernel,
        grid_spec=pltpu.PrefetchScalarGridSpec(
            num_scalar_prefetch=3,
            grid=(N_EXPERTS,),
            in_specs=[
                pl.BlockSpec(memory_space=pl.ANY),
                pl.BlockSpec((None, None, D_MODEL, 2 * D_EXPERT), wsel),
                pl.BlockSpec((None, None, 1, 2 * D_EXPERT), wsel),
                pl.BlockSpec((None, None, D_EXPERT, D_MODEL), wsel),
                pl.BlockSpec((None, None, 1, D_MODEL), wsel),
            ],
            out_specs=pl.BlockSpec(memory_space=pl.ANY),
            scratch_shapes=[
                pltpu.VMEM((D_MODEL, 2 * D_EXPERT), BF16),
                pltpu.VMEM((D_EXPERT, D_MODEL), BF16),
                pltpu.VMEM((2, EXPERT_BLOCK, D_MODEL), F32),
                pltpu.VMEM((2, EXPERT_BLOCK, D_MODEL), F32),
                pltpu.SemaphoreType.DMA((2,)),
                pltpu.SemaphoreType.DMA((2,)),
            ],
        ),
        out_shape=jax.ShapeDtypeStruct((n_rows, D_MODEL), F32),
        compiler_params=pltpu.CompilerParams(
            dimension_semantics=("arbitrary",), vmem_limit_bytes=VMEM_LIMIT),
        name="experts",
    )(first_block, num_blocks, nused, xs, w_up, b_up.reshape(DEPTH, N_EXPERTS, 1, 2 * D_EXPERT), w_down,
      b_down.reshape(DEPTH, N_EXPERTS, 1, D_MODEL))


def _combine_kernel(dest_ref, x_ref, gate_ref, lg_ref, lb_ref, ys_ref, o_ref, ybuf, sem):
    i = pl.program_id(0)
    n_steps = pl.num_programs(0)
    tile = x_ref.shape[0]
    n_tok = tile * n_steps

    def gather(step, slot):
        def issue(s, carry):
            for u in range(ROW_DMA_UNROLL):
                t = s * ROW_DMA_UNROLL + u
                for k in range(TOP_K):
                    d = dest_ref[k * n_tok + step * tile + t]
                    pltpu.make_async_copy(ys_ref.at[pl.ds(d, 1), :], ybuf.at[slot, k, pl.ds(t, 1), :],
                                          sem.at[slot]).start(priority=k % 2)
            return carry

        lax.fori_loop(0, tile // ROW_DMA_UNROLL, issue, 0)

    slot = lax.rem(i, 2)

    @pl.when(i == 0)
    def _():
        gather(0, 0)

    @pl.when(i + 1 < n_steps)
    def _():
        gather(i + 1, 1 - slot)

    for k in range(TOP_K):
        pltpu.make_async_copy(ys_ref.at[pl.ds(0, tile), :], ybuf.at[slot, k], sem.at[slot]).wait()

    ff = gate_ref[:, 0:1] * ybuf[slot, 0]
    for k in range(1, TOP_K):
        ff = ff + gate_ref[:, k:k + 1] * ybuf[slot, k]
    o_ref[...] = _layer_norm(DN_ALPHA * x_ref[...] + ff, lg_ref[...], lb_ref[...])


def _combine(x2d, ys, dest_flat, gates, ln_g, ln_b):
    n_tok = x2d.shape[0]
    tile = min(COMBINE_TILE, n_tok)
    return pl.pallas_call(
        _combine_kernel,
        grid_spec=pltpu.PrefetchScalarGridSpec(
            num_scalar_prefetch=1,
            grid=(n_tok // tile,),
            in_specs=[
                pl.BlockSpec((tile, D_MODEL), lambda i, *_: (i, 0)),
                pl.BlockSpec((tile, TOP_K), lambda i, *_: (i, 0)),
                pl.BlockSpec((1, D_MODEL), lambda i, *_: (0, 0)),
                pl.BlockSpec((1, D_MODEL), lambda i, *_: (0, 0)),
                pl.BlockSpec(memory_space=pl.ANY),
            ],
            out_specs=pl.BlockSpec((tile, D_MODEL), lambda i, *_: (i, 0)),
            scratch_shapes=[pltpu.VMEM((2, TOP_K, tile, D_MODEL), F32), pltpu.SemaphoreType.DMA((2,))],
        ),
        out_shape=jax.ShapeDtypeStruct(x2d.shape, F32),
        compiler_params=pltpu.CompilerParams(
            dimension_semantics=("arbitrary",), vmem_limit_bytes=VMEM_LIMIT),
        name="combine",
    )(dest_flat, x2d, gates, ln_g.reshape(1, -1), ln_b.reshape(1, -1), ys)


def _moe_layer(x2d, layer, router_w, router_b, w_up, b_up, w_down, b_down, ln_g, ln_b):
    gate_t, dest_t, layout = _route(x2d, router_w, router_b)
    dest_flat = dest_t.reshape(-1)
    nused = layout[0, LAYOUT_TOTAL_BLOCKS:LAYOUT_TOTAL_BLOCKS + 1]
    xs = _dispatch(x2d, dest_flat, layout[:, LAYOUT_PAD_LO], layout[:, LAYOUT_PAD_HI], nused)
    ys = _experts(xs, layout[:, LAYOUT_FIRST_BLOCK], layout[:, LAYOUT_NUM_BLOCKS], nused, layer,
                  w_up, b_up, w_down, b_down)
    return _combine(x2d, ys, dest_flat, gate_t.T, ln_g, ln_b)


def kernel(x, ev_w_in, ev_b_in, pool_w, pool_b, pool_scale, conv_w, conv_b, conv_ln_g, conv_ln_b, ev_w_out,
           ev_b_out, od_w_in, od_b_in, sgu_ln_g, sgu_ln_b, sgu_w, sgu_b, od_w_out, od_b_out, router_w, router_b,
           exp_w_up, exp_b_up, exp_w_down, exp_b_down, ln_g, ln_b):
    batch, seq, d = x.shape
    h = x.reshape(batch * seq, d)
    for layer in range(DEPTH):
        i = layer // 2
        if layer % 2 == 0:
            h = _even_layer(h, batch, seq, ev_w_in[i], ev_b_in[i], pool_w[i], pool_b[i], pool_scale[i],
                            conv_w[i], conv_b[i], conv_ln_g[i], conv_ln_b[i], ev_w_out[i], ev_b_out[i],
                            ln_g[layer, 0], ln_b[layer, 0])
        else:
            h = _odd_layer(h, od_w_in[i], od_b_in[i], sgu_ln_g[i], sgu_ln_b[i], sgu_w[i], sgu_b[i],
                           od_w_out[i], od_b_out[i], ln_g[layer, 0], ln_b[layer, 0])
        h = _moe_layer(h, layer, router_w[layer], router_b[layer], exp_w_up, exp_b_up, exp_w_down,
                       exp_b_down, ln_g[layer, 1], ln_b[layer, 1])
    return h.reshape(batch, seq, d)
```
